```python
import math
import jax
import jax.numpy as jnp
from jax import lax
import numpy as np

D_MODEL = 2048
BATCH = 4
SEQ = 4096
DEPTH = 1
DEC_BATCH = 8
DEC_SEQ = 32
PAST_LEN = 4096

CHUNK = 64
EPS = 1e-6
F32 = jnp.float32
R_HEADS = 8
R_DK = 128
R_DV = 256
R_QK = R_HEADS * R_DK
R_V = R_HEADS * R_DV
ROPE_BASE = 10000.0
S_EXPAND = 2
S_INNER = S_EXPAND * D_MODEL
S_HEADDIM = 64
S_HEADS = S_INNER // S_HEADDIM
S_STATE = 128
S_GROUPS = 8
S_HPG = S_HEADS // S_GROUPS
S_CONV = 4
S_BC = S_GROUPS * S_STATE
S_CONV_DIM = S_INNER + 2 * S_BC
N_EXPERTS = 32
TOP_K = 4
D_FF = D_MODEL
SWIGLU_ALPHA = 1.702
SWIGLU_LIMIT = 7.0
MOE_BLOCK = 128
PLE_DIM = 256
IN_SIZES = (R_QK, R_QK, R_V, R_V, S_INNER, S_CONV_DIM, S_HEADS, D_MODEL, D_MODEL)
IN_DIM = 2 * R_QK + 2 * R_V + S_INNER + S_CONV_DIM + S_HEADS + 2 * D_MODEL

kernel_name = "retention_ssd_gated_merge_moe_streaming_step"


def _in_offsets():
    return [int(o) for o in np.cumsum(IN_SIZES)[:-1]]


def _rmsnorm(x, g):
    xf = x.astype(F32)
    y = xf * lax.rsqrt(jnp.mean(xf * xf, axis=-1, keepdims=True) + EPS)
    return (y * g.astype(F32)).astype(x.dtype)


def _rope(x, pos):
    half = x.shape[-1] // 2
    inv = ROPE_BASE ** (-jnp.arange(half, dtype=F32) / half)
    ang = pos.astype(F32)[:, None] * inv[None, :]
    cos = jnp.cos(ang)[None, :, None, :]
    sin = jnp.sin(ang)[None, :, None, :]
    x1, x2 = x[..., :half], x[..., half:]
    return jnp.concatenate([x1 * cos - x2 * sin, x1 * sin + x2 * cos], axis=-1)


def _to_chunks(a, L):
    b, t = a.shape[0], a.shape[1]
    return jnp.moveaxis(a.reshape((b, t // L, L) + a.shape[2:]), 1, 0)


def _from_chunks(a):
    n, b, L = a.shape[0], a.shape[1], a.shape[2]
    return jnp.moveaxis(a, 0, 1).reshape((b, n * L) + a.shape[3:])


def _retention(q, k, v, s0, L):
    log_g = jnp.log1p(-jnp.exp2(-5.0 - jnp.arange(R_HEADS, dtype=F32)))
    idx = jnp.arange(L, dtype=F32)
    causal = idx[:, None] >= idx[None, :]
    rel = (idx[:, None] - idx[None, :])[None] * log_g[:, None, None]
    dmat = jnp.exp(jnp.where(causal[None], rel, -jnp.inf))
    xi = jnp.exp((idx[:, None] + 1.0) * log_g[None, :])
    zeta = jnp.exp((L - 1.0 - idx)[:, None] * log_g[None, :])
    g_chunk = jnp.exp(L * log_g)

    def step(s, inp):
        qc, kc, vc = inp
        sc = jnp.einsum('blhd,bmhd->bhlm', qc, kc) * dmat[None]
        o = (jnp.einsum('bhlm,bmhe->blhe', sc, vc)
             + jnp.einsum('blhd,bhde->blhe', qc, s) * xi[None, :, :, None])
        s = s * g_chunk[None, :, None, None] + jnp.einsum('bmhd,bmhe->bhde', kc * zeta[None, :, :, None], vc)
        return s, o

    s_new, o = lax.scan(step, s0, (_to_chunks(q, L), _to_chunks(k, L), _to_chunks(v, L)))
    return _from_chunks(o), s_new


def _ssd(xh, dt, a, bm, cm, h0, L):
    causal = jnp.tril(jnp.ones((L, L), dtype=bool))

    def step(h, inp):
        xc, dtc, bc, cc = inp
        acum = jnp.cumsum(dtc * a, axis=1)
        seg = acum[:, :, None] - acum[:, None, :]
        decay = jnp.exp(jnp.where(causal[None, :, :, None, None], seg, -jnp.inf))
        cb = jnp.einsum('blgn,bmgn->blmg', cc, bc)
        w = cb[..., None] * decay * dtc[:, None]
        y = jnp.einsum('blmgh,bmghp->blghp', w, xc)
        y = y + jnp.einsum('blgn,bghpn->blghp', cc, h) * jnp.exp(acum)[..., None]
        last = acum[:, -1]
        wk = jnp.exp(last[:, None] - acum) * dtc
        h = h * jnp.exp(last)[..., None, None] + jnp.einsum('blgh,blgn,blghp->bghpn', wk, bc, xc)
        return h, y

    h_new, y = lax.scan(step, h0, (_to_chunks(xh, L), _to_chunks(dt, L), _to_chunks(bm, L), _to_chunks(cm, L)))
    return _from_chunks(y), h_new


def _causal_conv(u, buf, w, b):
    t = u.shape[1]
    up = jnp.concatenate([buf.astype(u.dtype), u], axis=1)
    y = b + up[:, 0:t] * w[0]
    for j in range(1, S_CONV):
        y = y + up[:, j:j + t] * w[j]
    return y, up[:, up.shape[1] - (S_CONV - 1):]


def _moe(h, router_w, router_b, w_gu, b_gu, w_dn, b_dn):
    shp = h.shape
    ht = h.reshape(-1, D_MODEL)
    t = ht.shape[0]
    logits = ht.astype(F32) @ router_w.astype(F32) + router_b.astype(F32)
    top_val, top_idx = lax.top_k(logits, TOP_K)
    gate = jax.nn.softmax(top_val, axis=-1)
    n_assign = t * TOP_K
    flat_e = top_idx.reshape(-1).astype(jnp.int32)
    flat_tok = jnp.arange(n_assign, dtype=jnp.int32) // TOP_K
    flat_w = gate.reshape(-1)
    order = jnp.argsort(flat_e)
    se = flat_e[order]
    counts = jnp.bincount(flat_e, length=N_EXPERTS)
    padded = (counts + MOE_BLOCK - 1) // MOE_BLOCK * MOE_BLOCK
    pad_end = jnp.cumsum(padded)
    pad_start = pad_end - padded
    start = jnp.cumsum(counts) - counts
    dest = pad_start[se] + jnp.arange(n_assign, dtype=jnp.int32) - start[se]
    n_blocks = -(-n_assign // MOE_BLOCK) + N_EXPERTS
    n_rows = n_blocks * MOE_BLOCK
    row_tok = jnp.full((n_rows,), t, jnp.int32).at[dest].set(flat_tok[order])
    row_w = jnp.zeros((n_rows,), F32).at[dest].set(flat_w[order])
    blk_start = jnp.arange(n_blocks) * MOE_BLOCK
    blk_e = jnp.minimum(jnp.sum(pad_end[None, :] <= blk_start[:, None], axis=1), N_EXPERTS - 1)
    h_pad = jnp.concatenate([ht, jnp.zeros((1, D_MODEL), ht.dtype)], axis=0)

    def expert_block(args):
        toks, e = args
        xb = h_pad[toks]
        gu = xb @ w_gu[e] + b_gu[e]
        g_, u_ = jnp.split(gu, 2, axis=-1)
        g_ = jnp.minimum(g_, SWIGLU_LIMIT)
        u_ = jnp.clip(u_, -SWIGLU_LIMIT, SWIGLU_LIMIT)
        act = (u_ + 1.0) * (g_ * jax.nn.sigmoid(SWIGLU_ALPHA * g_))
        return act @ w_dn[e] + b_dn[e]

    out = lax.map(expert_block, (row_tok.reshape(n_blocks, MOE_BLOCK), blk_e))
    out = out.reshape(n_rows, D_MODEL) * row_w[:, None].astype(out.dtype)
    y = jnp.zeros((t + 1, D_MODEL), out.dtype).at[row_tok].add(out)[:t]
    return y.reshape(shp).astype(h.dtype)


def _layer(x, pe, pos, s_ret, s_ssm, s_conv, L, attn_norm, w_in, conv_w, conv_b, dt_bias, a_log,
           d_skip, ssm_norm, w_ret_o, w_ssm_o, w_out, ffn_norm, router_w, router_b, w_gu, b_gu,
           w_dn, b_dn, w_ple, w_ple_gate):
    b, t, _ = x.shape
    dty = x.dtype
    h = _rmsnorm(x, attn_norm)
    proj = h @ w_in
    q, k, v, g_ret, z, xbc, dt_raw, gate_r, gate_s = jnp.split(proj, _in_offsets(), axis=-1)
    q = _rope(q.reshape(b, t, R_HEADS, R_DK).astype(F32), pos)
    k = _rope(k.reshape(b, t, R_HEADS, R_DK).astype(F32), pos) * (R_DK ** -0.5)
    v = v.reshape(b, t, R_HEADS, R_DV).astype(F32)
    o_r, s_ret_new = _retention(q, k, v, s_ret.astype(F32), L)
    o_r = o_r * lax.rsqrt(jnp.mean(o_r * o_r, axis=-1, keepdims=True) + EPS)
    o_r = o_r.reshape(b, t, R_V) * jax.nn.silu(g_ret.astype(F32))
    xbc, s_conv_new = _causal_conv(xbc, s_conv, conv_w, conv_b)
    xbc = jax.nn.silu(xbc.astype(F32))
    xs, bm, cm = jnp.split(xbc, [S_INNER, S_INNER + S_BC], axis=-1)
    xh = xs.reshape(b, t, S_GROUPS, S_HPG, S_HEADDIM)
    dt = jax.nn.softplus(dt_raw.astype(F32) + dt_bias.astype(F32)).reshape(b, t, S_GROUPS, S_HPG)
    a = -jnp.exp(a_log.astype(F32)).reshape(S_GROUPS, S_HPG)
    h0 = s_ssm.astype(F32).reshape(b, S_GROUPS, S_HPG, S_HEADDIM, S_STATE)
    y, h_new = _ssd(xh, dt, a, bm.reshape(b, t, S_GROUPS, S_STATE), cm.reshape(b, t, S_GROUPS, S_STATE), h0, L)
    y = y + d_skip.astype(F32).reshape(S_GROUPS, S_HPG)[:, :, None] * xh
    y = y.reshape(b, t, S_INNER) * jax.nn.silu(z.astype(F32))
    yg = y.reshape(b, t, S_GROUPS, S_INNER // S_GROUPS)
    yg = yg * lax.rsqrt(jnp.mean(yg * yg, axis=-1, keepdims=True) + EPS)
    y = yg.reshape(b, t, S_INNER) * ssm_norm.astype(F32)
    u = (jax.nn.sigmoid(gate_r.astype(F32)) * (o_r.astype(dty) @ w_ret_o).astype(F32)
         + jax.nn.sigmoid(gate_s.astype(F32)) * (y.astype(dty) @ w_ssm_o).astype(F32))
    x = x + (u.astype(dty) @ w_out).astype(dty)
    x = x + _moe(_rmsnorm(x, ffn_norm), router_w, router_b, w_gu, b_gu, w_dn, b_dn)
    x = x + ((pe @ w_ple) * jax.nn.sigmoid(x @ w_ple_gate)).astype(dty)
    return x, s_ret_new, h_new.reshape(b, S_HEADS, S_HEADDIM, S_STATE), s_conv_new


def setup_inputs(seed: int = 0) -> dict:
    key = jax.random.key(seed)
    ks = jax.random.split(key, 32)

    def nrm(k, shape, scale):
        return jax.random.normal(k, shape, F32) * scale

    def gain(k, shape):
        return 1.0 + 0.01 * jax.random.normal(k, shape, F32)

    dt0 = jnp.exp(jax.random.uniform(ks[10], (DEPTH, S_HEADS), F32, math.log(1e-3), math.log(1e-1)))
    return {
        'x_prompt': nrm(ks[0], (BATCH, SEQ, D_MODEL), 1.0),
        'x_sample': nrm(ks[1], (DEC_BATCH, DEC_SEQ, D_MODEL), 1.0),
        'p_prompt': nrm(ks[2], (DEPTH, BATCH, SEQ, PLE_DIM), 1.0),
        'p_sample': nrm(ks[3], (DEPTH, DEC_BATCH, DEC_SEQ, PLE_DIM), 1.0),
        'state_ret': nrm(ks[4], (DEPTH, DEC_BATCH, R_HEADS, R_DK, R_DV), 0.5),
        'state_ssm': nrm(ks[5], (DEPTH, DEC_BATCH, S_HEADS, S_HEADDIM, S_STATE), 0.1),
        'state_conv': nrm(ks[6], (DEPTH, DEC_BATCH, S_CONV - 1, S_CONV_DIM), 1.0),
        'attn_norm': gain(ks[7], (DEPTH, D_MODEL)),
        'w_in': nrm(ks[8], (DEPTH, D_MODEL, IN_DIM), D_MODEL ** -0.5),
        'conv_w': nrm(ks[9], (DEPTH, S_CONV, S_CONV_DIM), S_CONV ** -0.5),
        'conv_b': nrm(ks[11], (DEPTH, S_CONV_DIM), 0.01),
        'dt_bias': dt0 + jnp.log(-jnp.expm1(-dt0)),
        'a_log': jnp.log(jax.random.uniform(ks[12], (DEPTH, S_HEADS), F32, 1.0, 16.0)),
        'd_skip': gain(ks[13], (DEPTH, S_HEADS)),
        'ssm_norm': gain(ks[14], (DEPTH, S_INNER)),
        'w_ret_o': nrm(ks[15], (DEPTH, R_V, D_MODEL), R_V ** -0.5),
        'w_ssm_o': nrm(ks[16], (DEPTH, S_INNER, D_MODEL), S_INNER ** -0.5),
        'w_out': nrm(ks[17], (DEPTH, D_MODEL, D_MODEL), D_MODEL ** -0.5),
        'ffn_norm': gain(ks[18], (DEPTH, D_MODEL)),
        'router_w': nrm(ks[19], (DEPTH, D_MODEL, N_EXPERTS), D_MODEL ** -0.5),
        'router_b': nrm(ks[20], (DEPTH, N_EXPERTS), 0.01),
        'w_gu': nrm(ks[21], (DEPTH, N_EXPERTS, D_MODEL, 2 * D_FF), D_MODEL ** -0.5),
        'b_gu': nrm(ks[22], (DEPTH, N_EXPERTS, 2 * D_FF), 0.01),
        'w_dn': nrm(ks[23], (DEPTH, N_EXPERTS, D_FF, D_MODEL), D_FF ** -0.5),
        'b_dn': nrm(ks[24], (DEPTH, N_EXPERTS, D_MODEL), 0.01),
        'w_ple': nrm(ks[25], (DEPTH, PLE_DIM, D_MODEL), PLE_DIM ** -0.5),
        'w_ple_gate': nrm(ks[26], (DEPTH, D_MODEL, D_MODEL), D_MODEL ** -0.5),
        'final_norm': gain(ks[27], (D_MODEL,)),
    }


def reference(x_prompt, x_sample, p_prompt, p_sample, state_ret, state_ssm, state_conv, attn_norm, w_in,
              conv_w, conv_b, dt_bias, a_log, d_skip, ssm_norm, w_ret_o, w_ssm_o, w_out, ffn_norm,
              router_w, router_b, w_gu, b_gu, w_dn, b_dn, w_ple, w_ple_gate, final_norm):
    bp, tp = x_prompt.shape[0], x_prompt.shape[1]
    ts = x_sample.shape[1]
    pos_p = jnp.arange(tp, dtype=jnp.int32)
    pos_s = PAST_LEN + jnp.arange(ts, dtype=jnp.int32)
    xp, xs = x_prompt, x_sample
    ret_p, ssm_p, conv_p, ret_s, ssm_s, conv_s = [], [], [], [], [], []
    for i in range(DEPTH):
        lw = (attn_norm[i], w_in[i], conv_w[i], conv_b[i], dt_bias[i], a_log[i], d_skip[i], ssm_norm[i],
              w_ret_o[i], w_ssm_o[i], w_out[i], ffn_norm[i], router_w[i], router_b[i], w_gu[i], b_gu[i],
              w_dn[i], b_dn[i], w_ple[i], w_ple_gate[i])
        xp, r, s, c = _layer(xp, p_prompt[i], pos_p,
                             jnp.zeros((bp, R_HEADS, R_DK, R_DV), F32),
                             jnp.zeros((bp, S_HEADS, S_HEADDIM, S_STATE), F32),
                             jnp.zeros((bp, S_CONV - 1, S_CONV_DIM), xp.dtype),
                             CHUNK, *lw)
        ret_p.append(r)
        ssm_p.append(s)
        conv_p.append(c)
        xs, r, s, c = _layer(xs, p_sample[i], pos_s, state_ret[i], state_ssm[i], state_conv[i], ts, *lw)
        ret_s.append(r)
        ssm_s.append(s)
        conv_s.append(c)
    y_prompt = _rmsnorm(xp, final_norm)
    y_sample = _rmsnorm(xs, final_norm)
    return (y_prompt, y_sample, jnp.stack(ret_p), jnp.stack(ssm_p), jnp.stack(conv_p),
            jnp.stack(ret_s), jnp.stack(ssm_s), jnp.stack(conv_s))
```

```python
import functools
import math

import jax
import jax.numpy as jnp
import numpy as np
from jax import lax
from jax.experimental import pallas as pl
from jax.experimental.pallas import tpu as pltpu

F32 = jnp.float32
BF16 = jnp.bfloat16

D_MODEL = 2048
EPS = 1e-6
PAST_LEN = 4096
R_HEADS = 8
R_DK = 128
R_DV = 256
R_QK = R_HEADS * R_DK
R_V = R_HEADS * R_DV
ROPE_BASE = 10000.0
S_INNER = 4096
S_HEADDIM = 64
S_HEADS = 64
S_STATE = 128
S_GROUPS = 8
S_HPG = S_HEADS // S_GROUPS
S_GW = S_HPG * S_HEADDIM
S_CONV = 4
S_BC = S_GROUPS * S_STATE
S_CONV_DIM = S_INNER + 2 * S_BC
N_EXPERTS = 32
TOP_K = 4
D_FF = D_MODEL
SWIGLU_ALPHA = 1.702
SWIGLU_LIMIT = 7.0
PLE_DIM = 256

LANES = 128
SUBLANES = 8
VMEM_LIMIT = 56 * 1024 * 1024

OFF_Q = 0
OFF_K = OFF_Q + R_QK
OFF_V = OFF_K + R_QK
OFF_G = OFF_V + R_V
OFF_Z = OFF_G + R_V
OFF_XBC = OFF_Z + S_INNER
OFF_GR = OFF_XBC + S_CONV_DIM
OFF_GS = OFF_GR + D_MODEL
OFF_DT = OFF_GS + D_MODEL
PROJ_N = OFF_DT + S_GROUPS * LANES

CHUNK = 128
MM_TM = 1280
MM_TN = 1024
ROW_TM = 640
FULLROW_TM = 320
MOE_TM = 512
MOE_TF = 1024


def _cparams(sem):
    return pltpu.CompilerParams(dimension_semantics=sem, vmem_limit_bytes=VMEM_LIMIT)


def _const_spec(shape):
    nd = len(shape)
    return pl.BlockSpec(shape, lambda *_: (0,) * nd, pipeline_mode=pl.Buffered(1))


def _sigmoid(x):
    return jax.nn.sigmoid(x)


def _silu(x):
    return x * jax.nn.sigmoid(x)


def _split3(x):
    hi = x.astype(BF16)
    r = x - hi.astype(F32)
    mid = r.astype(BF16)
    lo = (r - mid.astype(F32)).astype(BF16)
    return hi, mid, lo


def _dot(a, b):
    return jnp.dot(a, b, preferred_element_type=F32)


def _dot_nt(a, b):
    return lax.dot_general(a, b, (((1,), (1,)), ((), ())), preferred_element_type=F32)


def _dot_tn(a_f32, b):
    return _dot(a_f32.T.astype(BF16), b)


def _dot3_right(x, m01):
    hi, mid, lo = _split3(x)
    return _dot(hi, m01) + _dot(mid, m01) + _dot(lo, m01)


def _dot3_left(m01, x):
    hi, mid, lo = _split3(x)
    return _dot(m01, hi) + _dot(m01, mid) + _dot(m01, lo)


def _pad_rows(x, rows):
    if x.shape[0] == rows:
        return x
    return jnp.concatenate([x, jnp.zeros((rows - x.shape[0],) + x.shape[1:], x.dtype)], axis=0)


def _rmsnorm_kernel(x_ref, g_ref, o_ref):
    x = x_ref[...]
    y = x * lax.rsqrt(jnp.mean(x * x, axis=-1, keepdims=True) + EPS)
    o_ref[...] = (y * g_ref[...]).astype(o_ref.dtype)


def _rmsnorm_bf16(x, g):
    n, d = x.shape
    return pl.pallas_call(
        _rmsnorm_kernel,
        grid=(n // ROW_TM,),
        in_specs=[pl.BlockSpec((ROW_TM, d), lambda i: (i, 0)), _const_spec((1, d))],
        out_specs=pl.BlockSpec((ROW_TM, d), lambda i: (i, 0)),
        out_shape=jax.ShapeDtypeStruct((n, d), BF16),
        compiler_params=_cparams(("parallel",)),
        name="rmsnorm",
    )(x, g.reshape(1, d))


def _mm_kernel(a_ref, b_ref, o_ref):
    o_ref[...] = _dot(a_ref[...], b_ref[...]).astype(o_ref.dtype)


def _matmul(a, b, out_dtype, tm, tn, name):
    m, k = a.shape
    n = b.shape[1]
    return pl.pallas_call(
        _mm_kernel,
        grid=(m // tm, n // tn),
        in_specs=[pl.BlockSpec((tm, k), lambda i, j: (i, 0)),
                  pl.BlockSpec((k, tn), lambda i, j: (0, j))],
        out_specs=pl.BlockSpec((tm, tn), lambda i, j: (i, j)),
        out_shape=jax.ShapeDtypeStruct((m, n), out_dtype),
        compiler_params=_cparams(("parallel", "arbitrary")),
        name=name,
    )(a, b)


def _retention_kernel(q_ref, k_ref, v_ref, g_ref, cos_ref, sin_ref, d_ref, xi_ref, zeta_ref,
                      s0_ref, o_ref, s_out_ref, s_scr, *, rows, g_chunk):
    c = pl.program_id(1)

    @pl.when(c == 0)
    def _():
        s_scr[...] = s0_ref[...]

    cos2 = _pad_rows(cos_ref[...], CHUNK)
    sin2 = _pad_rows(sin_ref[...], CHUNK)
    scale = R_DK ** -0.5
    for h in range(R_HEADS):
        q = _pad_rows(q_ref[:, h * R_DK:(h + 1) * R_DK], CHUNK)
        k = _pad_rows(k_ref[:, h * R_DK:(h + 1) * R_DK], CHUNK)
        v = _pad_rows(v_ref[:, h * R_DV:(h + 1) * R_DV], CHUNK).astype(BF16)
        qr = q * cos2 + pltpu.roll(q, R_DK // 2, 1) * sin2
        kr = (k * cos2 + pltpu.roll(k, R_DK // 2, 1) * sin2) * scale
        qb = qr.astype(BF16)
        sc = _dot_nt(qb, kr.astype(BF16)) * d_ref[h]
        s = s_scr[h]
        o = _dot(sc.astype(BF16), v) + _dot(qb, s.astype(BF16)) * xi_ref[h]
        s_scr[h] = s * g_chunk[h] + _dot_tn(kr * zeta_ref[h], v)
        o = o * lax.rsqrt(jnp.mean(o * o, axis=-1, keepdims=True) + EPS)
        gate = _silu(_pad_rows(g_ref[:, h * R_DV:(h + 1) * R_DV], CHUNK))
        o_ref[:, h * R_DV:(h + 1) * R_DV] = (o * gate)[:rows].astype(o_ref.dtype)

    @pl.when(c == pl.num_programs(1) - 1)
    def _():
        s_out_ref[...] = s_scr[...]


def _retention_tables(pos, rows):
    half = R_DK // 2
    inv = ROPE_BASE ** (-jnp.arange(half, dtype=F32) / half)
    ang = pos.astype(F32)[:, None] * inv[None, :]
    cos, sin = jnp.cos(ang), jnp.sin(ang)
    cos2 = jnp.concatenate([cos, cos], axis=-1)
    sin2 = jnp.concatenate([-sin, sin], axis=-1)
    log_g = jnp.log1p(-jnp.exp2(-5.0 - jnp.arange(R_HEADS, dtype=F32)))
    idx = jnp.arange(CHUNK, dtype=F32)
    valid = idx < rows
    causal = (idx[:, None] >= idx[None, :]) & valid[None, :]
    rel = (idx[:, None] - idx[None, :])[None] * log_g[:, None, None]
    dmat = jnp.exp(jnp.where(causal[None], rel, -jnp.inf))
    xi = jnp.exp((idx[None, :] + 1.0) * log_g[:, None])
    zeta = jnp.where(valid[None, :], jnp.exp((rows - 1.0 - idx)[None, :] * log_g[:, None]), 0.0)
    xi = jnp.broadcast_to(xi[:, :, None], (R_HEADS, CHUNK, R_DV))
    zeta = jnp.broadcast_to(zeta[:, :, None], (R_HEADS, CHUNK, R_DK))
    g_chunk = tuple(float(math.exp(rows * math.log1p(-2.0 ** (-5.0 - h)))) for h in range(R_HEADS))
    return cos2, sin2, dmat, xi, zeta, g_chunk


def _retention(proj, s0, pos, row0, batch, seq, rows):
    n_chunks = seq // rows
    rb0 = row0 // rows
    cos2, sin2, dmat, xi, zeta, g_chunk = _retention_tables(pos, rows)

    def tok(width, off):
        return pl.BlockSpec((rows, width), lambda b, c: (rb0 + b * n_chunks + c, off // width))

    state_spec = pl.BlockSpec((None, R_HEADS, R_DK, R_DV), lambda b, c: (b, 0, 0, 0))
    kern = functools.partial(_retention_kernel, rows=rows, g_chunk=g_chunk)
    o, s_new = pl.pallas_call(
        kern,
        grid=(batch, n_chunks),
        in_specs=[tok(R_QK, OFF_Q), tok(R_QK, OFF_K), tok(R_V, OFF_V), tok(R_V, OFF_G),
                  pl.BlockSpec((rows, R_DK), lambda b, c: (c, 0)),
                  pl.BlockSpec((rows, R_DK), lambda b, c: (c, 0)),
                  _const_spec(dmat.shape), _const_spec(xi.shape), _const_spec(zeta.shape),
                  state_spec],
        out_specs=[pl.BlockSpec((rows, R_V), lambda b, c: (b * n_chunks + c, 0)), state_spec],
        out_shape=[jax.ShapeDtypeStruct((batch * seq, R_V), BF16),
                   jax.ShapeDtypeStruct((batch, R_HEADS, R_DK, R_DV), F32)],
        scratch_shapes=[pltpu.VMEM((R_HEADS, R_DK, R_DV), F32)],
        compiler_params=_cparams(("parallel", "arbitrary")),
        name="retention",
    )(proj, proj, proj, proj, cos2, sin2, dmat, xi, zeta, s0)
    return o, s_new


def _ssd_kernel(x_ref, b_ref, c_ref, z_ref, dt_ref, cwx_ref, cwb_ref, cwc_ref, cbx_ref, cbb_ref, cbc_ref,
                tx_ref, tb_ref, tc_ref, h0_ref, dtb_ref, alog_ref, dskip_ref, norm_ref,
                tril_ref, triu_ref, expand_ref, y_ref, h_out_ref, ex_scr, eb_scr, ec_scr, h_scr, *, rows):
    c = pl.program_id(2)
    tail = SUBLANES

    @pl.when(c == 0)
    def _():
        h_scr[...] = h0_ref[...].T
        if rows < CHUNK:
            ex_scr[...] = jnp.zeros_like(ex_scr)
            eb_scr[...] = jnp.zeros_like(eb_scr)
            ec_scr[...] = jnp.zeros_like(ec_scr)
        ex_scr[0:tail, :] = tx_ref[...]
        eb_scr[0:tail, :] = tb_ref[...]
        ec_scr[0:tail, :] = tc_ref[...]

    def conv(raw_ref, ext, w_ref, bias_ref):
        ext[tail:tail + rows, :] = raw_ref[...]
        y = bias_ref[...] + ext[tail - 3:tail - 3 + CHUNK, :] * w_ref[0:1, :]
        for j in range(1, S_CONV):
            y = y + ext[tail - 3 + j:tail - 3 + j + CHUNK, :] * w_ref[j:j + 1, :]
        ext[0:tail, :] = ext[rows:rows + tail, :]
        return _silu(y)

    x = conv(x_ref, ex_scr, cwx_ref, cbx_ref)
    bm_f = conv(b_ref, eb_scr, cwb_ref, cbb_ref)
    bm = bm_f.astype(BF16)
    cm = conv(c_ref, ec_scr, cwc_ref, cbc_ref).astype(BF16)

    dt_in = _pad_rows(dt_ref[...], CHUNK) + dtb_ref[...]
    dt = jnp.maximum(dt_in, 0.0) + jnp.log1p(jnp.exp(-jnp.abs(dt_in)))
    if rows < CHUNK:
        row_id = lax.broadcasted_iota(jnp.int32, (CHUNK, LANES), 0)
        dt = jnp.where(row_id < rows, dt, 0.0)
    da = dt * (-jnp.exp(alog_ref[...]))
    acum = _dot3_left(tril_ref[...], da)
    acum_t = _dot3_right(da.T, triu_ref[...])
    dt_t = dt.T
    last = acum[CHUNK - 1:CHUNK, :]

    cb = _dot_nt(cm, bm)
    li = lax.broadcasted_iota(jnp.int32, (CHUNK, CHUNK), 0)
    mi = lax.broadcasted_iota(jnp.int32, (CHUNK, CHUNK), 1)
    causal = li >= mi
    head_of_lane = lax.shift_right_logical(lax.broadcasted_iota(jnp.int32, (CHUNK, S_GW), 1),
                                           int(math.log2(S_HEADDIM)))
    xb = x.astype(BF16)
    zero_b = jnp.zeros_like(xb)
    y = jnp.zeros((CHUNK, S_GW), F32)
    for j in range(S_HPG):
        seg = acum[:, j:j + 1] - acum_t[j:j + 1, :]
        w = cb * jnp.exp(jnp.where(causal, seg, -1e30)) * dt_t[j:j + 1, :]
        y = y + _dot(w.astype(BF16), jnp.where(head_of_lane == j, xb, zero_b))

    expand = expand_ref[...]
    h_t = h_scr[...]
    y = y + _dot(cm, h_t.astype(BF16)) * _dot3_right(jnp.exp(acum), expand)
    wk = _dot3_right(jnp.exp(last - acum) * dt, expand)
    e_last = _dot3_right(jnp.broadcast_to(jnp.exp(last), (SUBLANES, LANES)), expand)[0:1, :]
    h_scr[...] = h_t * e_last + _dot_tn(bm_f, (x * wk).astype(BF16))

    y = y + dskip_ref[...] * x
    y = y * _silu(_pad_rows(z_ref[...], CHUNK))
    y = y * lax.rsqrt(jnp.mean(y * y, axis=-1, keepdims=True) + EPS) * norm_ref[...]
    y_ref[...] = y[:rows].astype(y_ref.dtype)

    @pl.when(c == pl.num_programs(2) - 1)
    def _():
        h_out_ref[...] = h_scr[...].T


def _ssd(proj, conv_tail, h0, row0, batch, seq, rows, conv_w, conv_b, dtb, alog, dskip, norm):
    n_chunks = seq // rows
    rb0 = row0 // rows
    idx = np.arange(CHUNK)
    tril = jnp.asarray(idx[:, None] >= idx[None, :], BF16)
    triu = jnp.asarray(idx[:, None] <= idx[None, :], BF16)
    lane = np.arange(S_GW)
    expand = jnp.asarray((np.arange(LANES)[:, None] == (lane // S_HEADDIM)[None, :]), BF16)

    def tok(width, off):
        return pl.BlockSpec((rows, width), lambda b, g, c: (rb0 + b * n_chunks + c, off // width + g))

    def chan(nrow, width, off):
        return pl.BlockSpec((nrow, width), lambda b, g, c: (0, off // width + g))

    def tailspec(width, off):
        return pl.BlockSpec((None, SUBLANES, width), lambda b, g, c: (b, 0, off // width + g))

    def per_group(width):
        return pl.BlockSpec((None, 1, width), lambda b, g, c: (g, 0, 0))

    state_spec = pl.BlockSpec((None, None, S_GW, S_STATE), lambda b, g, c: (b, g, 0, 0))
    kern = functools.partial(_ssd_kernel, rows=rows)
    y, h_new = pl.pallas_call(
        kern,
        grid=(batch, S_GROUPS, n_chunks),
        in_specs=[tok(S_GW, OFF_XBC), tok(S_STATE, OFF_XBC + S_INNER), tok(S_STATE, OFF_XBC + S_INNER + S_BC),
                  tok(S_GW, OFF_Z), tok(LANES, OFF_DT),
                  chan(S_CONV, S_GW, 0), chan(S_CONV, S_STATE, S_INNER), chan(S_CONV, S_STATE, S_INNER + S_BC),
                  chan(1, S_GW, 0), chan(1, S_STATE, S_INNER), chan(1, S_STATE, S_INNER + S_BC),
                  tailspec(S_GW, 0), tailspec(S_STATE, S_INNER), tailspec(S_STATE, S_INNER + S_BC),
                  state_spec, per_group(LANES), per_group(LANES), per_group(S_GW), per_group(S_GW),
                  _const_spec(tril.shape), _const_spec(triu.shape), _const_spec(expand.shape)],
        out_specs=[pl.BlockSpec((rows, S_GW), lambda b, g, c: (b * n_chunks + c, g)), state_spec],
        out_shape=[jax.ShapeDtypeStruct((batch * seq, S_INNER), BF16),
                   jax.ShapeDtypeStruct((batch, S_GROUPS, S_GW, S_STATE), F32)],
        scratch_shapes=[pltpu.VMEM((CHUNK + 2 * SUBLANES, S_GW), F32),
                        pltpu.VMEM((CHUNK + 2 * SUBLANES, S_STATE), F32),
                        pltpu.VMEM((CHUNK + 2 * SUBLANES, S_STATE), F32),
                        pltpu.VMEM((S_STATE, S_GW), F32)],
        compiler_params=_cparams(("parallel", "parallel", "arbitrary")),
        name="ssd",
    )(proj, proj, proj, proj, proj, conv_w, conv_w, conv_w, conv_b, conv_b, conv_b,
      conv_tail, conv_tail, conv_tail, h0, dtb, alog, dskip, norm, tril, triu, expand)
    return y, h_new


def _merge_kernel(o_ref, y_ref, gr_ref, gs_ref, wr_ref, ws_ref, u_ref):
    a = _dot(o_ref[...], wr_ref[...])
    b = _dot(y_ref[...], ws_ref[...])
    u_ref[...] = (_sigmoid(gr_ref[...]) * a + _sigmoid(gs_ref[...]) * b).astype(u_ref.dtype)


def _merge(o_r, y_s, proj, w_ret_o, w_ssm_o):
    n = o_r.shape[0]
    tm, tn = ROW_TM, 512
    return pl.pallas_call(
        _merge_kernel,
        grid=(n // tm, D_MODEL // tn),
        in_specs=[pl.BlockSpec((tm, R_V), lambda i, j: (i, 0)),
                  pl.BlockSpec((tm, S_INNER), lambda i, j: (i, 0)),
                  pl.BlockSpec((tm, tn), lambda i, j: (i, OFF_GR // tn + j)),
                  pl.BlockSpec((tm, tn), lambda i, j: (i, OFF_GS // tn + j)),
                  pl.BlockSpec((R_V, tn), lambda i, j: (0, j)),
                  pl.BlockSpec((S_INNER, tn), lambda i, j: (0, j))],
        out_specs=pl.BlockSpec((tm, tn), lambda i, j: (i, j)),
        out_shape=jax.ShapeDtypeStruct((n, D_MODEL), BF16),
        compiler_params=_cparams(("parallel", "arbitrary")),
        name="merge",
    )(o_r, y_s, proj, proj, w_ret_o, w_ssm_o)


def _outproj_router_kernel(u_ref, x_ref, w_ref, g_ref, rw_ref, rb_ref, xm_ref, h_ref, idx_ref, gate_ref):
    xm = x_ref[...] + _dot(u_ref[...], w_ref[...])
    xm_ref[...] = xm
    h = xm * lax.rsqrt(jnp.mean(xm * xm, axis=-1, keepdims=True) + EPS) * g_ref[...]
    h_ref[...] = h.astype(h_ref.dtype)
    hi, mid, lo = _split3(h)
    rw = rw_ref[...]
    logits = (_dot(hi, rw[0]) + (_dot(hi, rw[1]) + _dot(mid, rw[0]))
              + (_dot(hi, rw[2]) + _dot(mid, rw[1]) + _dot(lo, rw[0]))) + rb_ref[...]
    lane = lax.broadcasted_iota(jnp.int32, logits.shape, 1).astype(F32)
    neg = jnp.float32(-jnp.inf)
    work = jnp.where(lane < N_EXPERTS, logits, neg)
    idx_out = jnp.zeros(logits.shape, F32)
    val_out = jnp.full(logits.shape, neg, F32)
    for kk in range(TOP_K):
        m = jnp.max(work, axis=-1, keepdims=True)
        sel = jnp.min(jnp.where(work == m, lane, float(LANES)), axis=-1, keepdims=True)
        idx_out = jnp.where(lane == kk, sel, idx_out)
        val_out = jnp.where(lane == kk, m, val_out)
        work = jnp.where(lane == sel, neg, work)
    top = jnp.max(val_out, axis=-1, keepdims=True)
    ex = jnp.exp(val_out - top)
    gate_ref[...] = ex / jnp.sum(ex, axis=-1, keepdims=True)
    idx_ref[...] = idx_out.astype(jnp.int32)


def _outproj_router(u, x, w_out, ffn_norm, rw3, rb):
    n = u.shape[0]
    tm = FULLROW_TM
    row = lambda width: pl.BlockSpec((tm, width), lambda i: (i, 0))
    return pl.pallas_call(
        _outproj_router_kernel,
        grid=(n // tm,),
        in_specs=[row(D_MODEL), row(D_MODEL), _const_spec((D_MODEL, D_MODEL)), _const_spec((1, D_MODEL)),
                  _const_spec((3, D_MODEL, LANES)), _const_spec((1, LANES))],
        out_specs=[row(D_MODEL), row(D_MODEL), row(LANES), row(LANES)],
        out_shape=[jax.ShapeDtypeStruct((n, D_MODEL), F32), jax.ShapeDtypeStruct((n, D_MODEL), BF16),
                   jax.ShapeDtypeStruct((n, LANES), jnp.int32), jax.ShapeDtypeStruct((n, LANES), F32)],
        compiler_params=_cparams(("parallel",)),
        name="outproj_router",
    )(u, x, w_out, ffn_norm, rw3, rb)


def _expert_kernel(blk_e_ref, n_used_ref, x_ref, wg_ref, wu_ref, bg_ref, bu_ref, wd_ref, bd_ref, rw_ref, o_ref):
    i = pl.program_id(0)
    f = pl.program_id(1)

    @pl.when(i < n_used_ref[0])
    def _():
        x = x_ref[...]
        g = _dot(x, wg_ref[...]) + bg_ref[...]
        u = _dot(x, wu_ref[...]) + bu_ref[...]
        g = jnp.minimum(g, SWIGLU_LIMIT)
        u = jnp.clip(u, -SWIGLU_LIMIT, SWIGLU_LIMIT)
        act = (u + 1.0) * (g * _sigmoid(SWIGLU_ALPHA * g))
        part = _dot(act.astype(BF16), wd_ref[...])

        @pl.when(f == 0)
        def _():
            o_ref[...] = part + bd_ref[...]

        @pl.when(f > 0)
        def _():
            o_ref[...] += part

        @pl.when(f == pl.num_programs(1) - 1)
        def _():
            o_ref[...] = o_ref[...] * rw_ref[...]


def _experts(x_sorted, row_w, blk_e, n_used, w_gu, b_gu, w_dn, b_dn):
    n_rows = x_sorted.shape[0]
    n_blocks = n_rows // MOE_TM
    nf = D_FF // MOE_TF

    def f_eff(i, f, nu):
        return jnp.where(i < nu[0], f, nf - 1)

    grid_spec = pltpu.PrefetchScalarGridSpec(
        num_scalar_prefetch=2,
        grid=(n_blocks, nf),
        in_specs=[
            pl.BlockSpec((MOE_TM, D_MODEL), lambda i, f, be, nu: (i, 0)),
            pl.BlockSpec((None, D_MODEL, MOE_TF), lambda i, f, be, nu: (be[i], 0, f_eff(i, f, nu))),
            pl.BlockSpec((None, D_MODEL, MOE_TF), lambda i, f, be, nu: (be[i], 0, nf + f_eff(i, f, nu))),
            pl.BlockSpec((None, 1, MOE_TF), lambda i, f, be, nu: (be[i], 0, f_eff(i, f, nu))),
            pl.BlockSpec((None, 1, MOE_TF), lambda i, f, be, nu: (be[i], 0, nf + f_eff(i, f, nu))),
            pl.BlockSpec((None, MOE_TF, D_MODEL), lambda i, f, be, nu: (be[i], f_eff(i, f, nu), 0)),
            pl.BlockSpec((None, 1, D_MODEL), lambda i, f, be, nu: (be[i], 0, 0)),
            pl.BlockSpec((MOE_TM, 1), lambda i, f, be, nu: (i, 0)),
        ],
        out_specs=pl.BlockSpec((MOE_TM, D_MODEL), lambda i, f, be, nu: (i, 0)),
    )
    return pl.pallas_call(
        _expert_kernel,
        grid_spec=grid_spec,
        out_shape=jax.ShapeDtypeStruct((n_rows, D_MODEL), F32),
        compiler_params=_cparams(("arbitrary", "arbitrary")),
        name="experts",
    )(blk_e, n_used, x_sorted, w_gu, w_gu, b_gu, b_gu, w_dn, b_dn, row_w)


def _ple_final_kernel(xm_ref, ym_ref, pe_ref, wp_ref, wg_ref, fn_ref, o_ref):
    x = xm_ref[...] + ym_ref[...]
    emb = _dot(pe_ref[...].astype(BF16), wp_ref[...])
    gate = _sigmoid(_dot(x.astype(BF16), wg_ref[...]))
    x = x + emb * gate
    o_ref[...] = x * lax.rsqrt(jnp.mean(x * x, axis=-1, keepdims=True) + EPS) * fn_ref[...]


def _ple_final(x_mid, y_moe, pe, w_ple, w_ple_gate, final_norm):
    n = x_mid.shape[0]
    tm = FULLROW_TM
    row = lambda width: pl.BlockSpec((tm, width), lambda i: (i, 0))
    return pl.pallas_call(
        _ple_final_kernel,
        grid=(n // tm,),
        in_specs=[row(D_MODEL), row(D_MODEL), row(PLE_DIM), _const_spec((PLE_DIM, D_MODEL)),
                  _const_spec((D_MODEL, D_MODEL)), _const_spec((1, D_MODEL))],
        out_specs=row(D_MODEL),
        out_shape=jax.ShapeDtypeStruct((n, D_MODEL), F32),
        compiler_params=_cparams(("parallel",)),
        name="ple_final",
    )(x_mid, y_moe, pe, w_ple, w_ple_gate, final_norm)


def _moe_plan(top_idx, gate, n_tok):
    n_assign = n_tok * TOP_K
    flat_e = top_idx.reshape(-1)
    flat_w = gate.reshape(-1)
    order = jnp.argsort(flat_e, stable=True)
    se = flat_e[order]
    counts = jnp.bincount(flat_e, length=N_EXPERTS)
    padded = (counts + MOE_TM - 1) // MOE_TM * MOE_TM
    pad_end = jnp.cumsum(padded)
    pad_start = pad_end - padded
    start = jnp.cumsum(counts) - counts
    dest = (pad_start[se] + jnp.arange(n_assign, dtype=jnp.int32) - start[se]).astype(jnp.int32)
    n_blocks = n_assign // MOE_TM + N_EXPERTS
    n_rows = n_blocks * MOE_TM
    row_tok = jnp.full((n_rows,), n_tok, jnp.int32).at[dest].set((order // TOP_K).astype(jnp.int32))
    row_w = jnp.zeros((n_rows,), F32).at[dest].set(flat_w[order])
    blk_start = jnp.arange(n_blocks, dtype=jnp.int32) * MOE_TM
    blk_e = jnp.minimum(jnp.sum(pad_end[None, :] <= blk_start[:, None], axis=1), N_EXPERTS - 1).astype(jnp.int32)
    n_used = (pad_end[-1] // MOE_TM).astype(jnp.int32).reshape(1)
    pos = jnp.zeros((n_assign,), jnp.int32).at[order].set(dest).reshape(n_tok, TOP_K)
    return row_tok, row_w.reshape(n_rows, 1), blk_e, n_used, pos


def kernel(x_prompt, x_sample, p_prompt, p_sample, state_ret, state_ssm, state_conv, attn_norm, w_in, conv_w, conv_b, dt_bias, a_log, d_skip, ssm_norm, w_ret_o, w_ssm_o, w_out, ffn_norm, router_w, router_b, w_gu, b_gu, w_dn, b_dn, w_ple, w_ple_gate, final_norm):
    bp, tp, _ = x_prompt.shape
    bs, ts, _ = x_sample.shape
    n_p, n_s = bp * tp, bs * ts
    n_tok = n_p + n_s
    assert n_tok % MM_TM == 0 and n_tok % ROW_TM == 0 and tp % CHUNK == 0 and ts <= CHUNK and n_p % ts == 0
    assert w_in.shape[0] == 1, "single-layer trunk"

    x = jnp.concatenate([x_prompt.reshape(n_p, D_MODEL), x_sample.reshape(n_s, D_MODEL)], axis=0)
    pe = jnp.concatenate([p_prompt[0].reshape(n_p, PLE_DIM), p_sample[0].reshape(n_s, PLE_DIM)], axis=0)

    wi = w_in[0]
    o_dt = R_QK * 2 + R_V * 2 + S_INNER + S_CONV_DIM
    w_dt = wi[:, o_dt:o_dt + S_HEADS].reshape(D_MODEL, S_GROUPS, S_HPG)
    w_dt = jnp.pad(w_dt, ((0, 0), (0, 0), (0, LANES - S_HPG))).reshape(D_MODEL, S_GROUPS * LANES)
    w_proj = jnp.concatenate([wi[:, :o_dt], wi[:, o_dt + S_HEADS:], w_dt], axis=1).astype(BF16)

    def per_group_lanes(v):
        return jnp.pad(v.reshape(S_GROUPS, 1, S_HPG), ((0, 0), (0, 0), (0, LANES - S_HPG)))

    dtb = per_group_lanes(dt_bias[0])
    alog = per_group_lanes(a_log[0])
    dskip = jnp.repeat(d_skip[0], S_HEADDIM).reshape(S_GROUPS, 1, S_GW)
    norm = ssm_norm[0].reshape(S_GROUPS, 1, S_GW)

    h = _rmsnorm_bf16(x, attn_norm[0])
    proj = _matmul(h, w_proj, F32, MM_TM, MM_TN, "in_proj")

    xbc_p = proj[:n_p, OFF_XBC:OFF_XBC + S_CONV_DIM].reshape(bp, tp, S_CONV_DIM)
    xbc_s = proj[n_p:, OFF_XBC:OFF_XBC + S_CONV_DIM].reshape(bs, ts, S_CONV_DIM)
    conv_p = xbc_p[:, tp - (S_CONV - 1):]
    conv_s = xbc_s[:, ts - (S_CONV - 1):]

    pos_p = jnp.arange(tp, dtype=jnp.int32)
    pos_s = PAST_LEN + jnp.arange(ts, dtype=jnp.int32)
    o_p, ret_p = _retention(proj, jnp.zeros((bp, R_HEADS, R_DK, R_DV), F32), pos_p, 0, bp, tp, CHUNK)
    o_s, ret_s = _retention(proj, state_ret[0].astype(F32), pos_s, n_p, bs, ts, ts)

    def conv_tail(buf):
        return jnp.pad(buf, ((0, 0), (SUBLANES - (S_CONV - 1), 0), (0, 0)))

    def ssd_state_in(s):
        return s.astype(F32).reshape(s.shape[0], S_GROUPS, S_GW, S_STATE)

    ssd_w = (conv_w[0], conv_b[0].reshape(1, S_CONV_DIM), dtb, alog, dskip, norm)
    y_p, ssm_p = _ssd(proj, conv_tail(jnp.zeros((bp, S_CONV - 1, S_CONV_DIM), F32)),
                      ssd_state_in(jnp.zeros((bp, S_HEADS, S_HEADDIM, S_STATE), F32)), 0, bp, tp, CHUNK, *ssd_w)
    y_s, ssm_s = _ssd(proj, conv_tail(state_conv[0]), ssd_state_in(state_ssm[0]), n_p, bs, ts, ts, *ssd_w)

    o_r = jnp.concatenate([o_p, o_s], axis=0)
    y_ssd = jnp.concatenate([y_p, y_s], axis=0)
    u = _merge(o_r, y_ssd, proj, w_ret_o[0].astype(BF16), w_ssm_o[0].astype(BF16))
    rw = jnp.pad(router_w[0].astype(F32), ((0, 0), (0, LANES - N_EXPERTS)))
    rw3 = jnp.stack(_split3(rw))
    rb = jnp.pad(router_b[0].astype(F32), (0, LANES - N_EXPERTS)).reshape(1, LANES)
    x_mid, h2, top_idx, gate = _outproj_router(u, x, w_out[0].astype(BF16), ffn_norm[0].reshape(1, D_MODEL), rw3, rb)

    row_tok, row_w, blk_e, n_used, pos = _moe_plan(top_idx[:, :TOP_K], gate[:, :TOP_K], n_tok)
    h2_pad = jnp.concatenate([h2, jnp.zeros((1, D_MODEL), h2.dtype)], axis=0)
    x_sorted = jnp.take(h2_pad, row_tok, axis=0)
    out_sorted = _experts(x_sorted, row_w, blk_e, n_used, w_gu[0].astype(BF16),
                          b_gu[0].reshape(N_EXPERTS, 1, 2 * D_FF), w_dn[0].astype(BF16),
                          b_dn[0].reshape(N_EXPERTS, 1, D_MODEL))
    y_moe = jnp.sum(jnp.take(out_sorted, pos, axis=0), axis=1)

    y = _ple_final(x_mid, y_moe, pe, w_ple[0].astype(BF16), w_ple_gate[0].astype(BF16),
                   final_norm.reshape(1, D_MODEL))

    y_prompt = y[:n_p].reshape(bp, tp, D_MODEL)
    y_sample = y[n_p:].reshape(bs, ts, D_MODEL)
    ssm_shape = (S_HEADS, S_HEADDIM, S_STATE)
    return (y_prompt, y_sample, ret_p[None], ssm_p.reshape((bp,) + ssm_shape)[None], conv_p[None],
            ret_s[None], ssm_s.reshape((bs,) + ssm_shape)[None], conv_s[None])
```
